```python
import jax, jax.numpy as jnp
from jax import lax
import numpy as np

D_MODEL = 2048
BATCH = 2
SEQ = 4096
DEPTH = 1
DEC_BATCH = 8
DEC_SEQ = 4
PAST_LEN = 16384
PAGE_SIZE = 128

N_META = 16
D_MIX = D_MODEL
D_ATT = D_MIX // 2
D_CONV = D_MIX - D_ATT
HEAD_DIM = 128
N_HEADS = D_ATT // HEAD_DIM
CONV_W = 3
CONV_STATE = CONV_W - 1
Q_BLOCK = 128
N_EXPERTS = 32
TOP_K = 4
D_FF = D_MODEL
SWIGLU_LIMIT = 7.0
SWIGLU_ALPHA = 1.702
FORGET_BIAS_INIT = 2.0
LN_EPS = 1e-5
DN_ALPHA = (2 * DEPTH) ** 0.25
DN_BETA = (8 * DEPTH) ** -0.25
ATT_SCALE = HEAD_DIM ** -0.5
N_IN = 3 * D_ATT + N_HEADS + 3 * D_CONV
SPLITS = (D_ATT, 2 * D_ATT, 3 * D_ATT, 3 * D_ATT + N_HEADS,
          3 * D_ATT + N_HEADS + D_CONV, 3 * D_ATT + N_HEADS + 2 * D_CONV)

kernel_name = "hymba_fox_shortconv_moe_step"


def _layer_norm(x, g, b):
    xf = x.astype(jnp.float32)
    mu = jnp.mean(xf, axis=-1, keepdims=True)
    var = jnp.mean(jnp.square(xf - mu), axis=-1, keepdims=True)
    return ((xf - mu) * lax.rsqrt(var + LN_EPS) * g + b).astype(x.dtype)


def _project(h, w_in, b_forget):
    z = h @ w_in
    q, k, v, fl, bg, cg, hc = jnp.split(z, SPLITS, axis=-1)
    heads = h.shape[:-1] + (N_HEADS, HEAD_DIM)
    logf = jax.nn.log_sigmoid(fl.astype(jnp.float32) + b_forget.astype(jnp.float32))
    return q.reshape(heads), k.reshape(heads), v.reshape(heads), logf, bg, cg * hc


def _fox_attend(q, k, v, cq, ck, q_pos, k_pos):
    s = jnp.einsum('bqhd,bkhd->bhqk', q, k, preferred_element_type=jnp.float32) * ATT_SCALE
    s = s + (jnp.swapaxes(cq, 1, 2)[:, :, :, None] - jnp.swapaxes(ck, 1, 2)[:, :, None, :])
    mask = k_pos[None, :] <= q_pos[:, None]
    s = jnp.where(mask[None, None], s, -jnp.inf)
    p = jax.nn.softmax(s, axis=-1)
    o = jnp.einsum('bhqk,bkhd->bqhd', p.astype(v.dtype), v, preferred_element_type=jnp.float32)
    return o.astype(q.dtype)


def _prompt_attention(q, k, v, csum):
    b, L = q.shape[:2]
    lead = (Q_BLOCK - N_META % Q_BLOCK) % Q_BLOCK
    n_blk = -(-(lead + L) // Q_BLOCK)
    tail = n_blk * Q_BLOCK - lead - L
    q_pad = jnp.pad(q, ((0, 0), (lead, tail), (0, 0), (0, 0)))
    c_pad = jnp.pad(csum, ((0, 0), (lead, tail), (0, 0)))
    q_pos = jnp.clip(jnp.arange(n_blk * Q_BLOCK) - lead, 0, L - 1)
    k_pos = jnp.arange(L)
    qb = jnp.swapaxes(q_pad.reshape(b, n_blk, Q_BLOCK, N_HEADS, HEAD_DIM), 0, 1)
    cb = jnp.swapaxes(c_pad.reshape(b, n_blk, Q_BLOCK, N_HEADS), 0, 1)
    pb = q_pos.reshape(n_blk, Q_BLOCK)
    out = lax.map(lambda a: _fox_attend(a[0], k, v, a[1], csum, a[2], k_pos), (qb, cb, pb))
    out = jnp.swapaxes(out, 0, 1).reshape(b, n_blk * Q_BLOCK, D_ATT)
    return out[:, lead:lead + L]


def _causal_conv(u_ext, conv_w, t):
    y = u_ext[:, 0:t] * conv_w[0]
    for j in range(1, CONV_W):
        y = y + u_ext[:, j:j + t] * conv_w[j]
    return y


def _moe(x, w_router, b_router, w_gate, b_gate, w_up, b_up, w_down, b_down):
    T, D = x.shape
    logits = (x @ w_router).astype(jnp.float32) + b_router.astype(jnp.float32)
    top_val, top_idx = lax.top_k(logits, TOP_K)
    gates = jax.nn.softmax(top_val, axis=-1)
    A = T * TOP_K
    blk = max(8, min(128, A // N_EXPERTS))
    flat_e = top_idx.reshape(-1).astype(jnp.int32)
    flat_tok = jnp.arange(A, dtype=jnp.int32) // TOP_K
    flat_g = gates.reshape(-1)
    order = jnp.argsort(flat_e)
    se = flat_e[order]
    counts = jnp.bincount(flat_e, length=N_EXPERTS)
    pcounts = (counts + blk - 1) // blk * blk
    ustart = jnp.cumsum(counts) - counts
    pend = jnp.cumsum(pcounts)
    pstart = pend - pcounts
    dest = pstart[se] + jnp.arange(A, dtype=jnp.int32) - ustart[se]
    n_blocks = -(-(A + N_EXPERTS * (blk - 1)) // blk)
    P = n_blocks * blk
    row_tok = jnp.full((P,), T, jnp.int32).at[dest].set(flat_tok[order])
    row_g = jnp.zeros((P,), jnp.float32).at[dest].set(flat_g[order])
    blk_e = jnp.minimum(jnp.searchsorted(pend, jnp.arange(n_blocks) * blk, side='right'), N_EXPERTS - 1)
    x_ext = jnp.concatenate([x, jnp.zeros((1, D), x.dtype)], axis=0)
    xb = x_ext[row_tok].reshape(n_blocks, blk, D)

    def expert_block(args):
        xe, e = args
        g = jnp.minimum(xe @ w_gate[e] + b_gate[e], SWIGLU_LIMIT)
        u = jnp.clip(xe @ w_up[e] + b_up[e], -SWIGLU_LIMIT, SWIGLU_LIMIT)
        hid = (u + 1.0) * (g * jax.nn.sigmoid(SWIGLU_ALPHA * g))
        return hid @ w_down[e] + b_down[e]

    yb = lax.map(expert_block, (xb, blk_e)).reshape(P, D)
    y = jnp.zeros((T + 1, D), jnp.float32).at[row_tok].add(yb.astype(jnp.float32) * row_g[:, None])
    return y[:T].astype(x.dtype)


def _channel_stage(h, mix, ln1_g, ln1_b, moe_w, ln2_g, ln2_b):
    x1 = _layer_norm(DN_ALPHA * h + mix, ln1_g, ln1_b)
    b, t, d = x1.shape
    f = _moe(x1.reshape(b * t, d), *moe_w).reshape(b, t, d)
    return _layer_norm(DN_ALPHA * x1 + f, ln2_g, ln2_b)


def _layer(hp, hs, cache_k, cache_v, cache_logf, state_conv, page_table, w_in, b_forget, conv_w,
           w_out, ln1_g, ln1_b, w_router, b_router, w_gate, b_gate, w_up, b_up, w_down, b_down,
           ln2_g, ln2_b):
    moe_w = (w_router, b_router, w_gate, b_gate, w_up, b_up, w_down, b_down)
    L = hp.shape[1]
    qp, kp, vp, lfp, bgp, up = _project(hp, w_in, b_forget)
    att_p = _prompt_attention(qp, kp, vp, jnp.cumsum(lfp, axis=1))
    up_ext = jnp.pad(up, ((0, 0), (CONV_STATE, 0), (0, 0)))
    conv_p = bgp * _causal_conv(up_ext, conv_w, L)
    mix_p = jnp.concatenate([att_p, conv_p], axis=-1) @ w_out
    hp_new = _channel_stage(hp, mix_p, ln1_g, ln1_b, moe_w, ln2_g, ln2_b)
    cp = up_ext[:, -CONV_STATE:]
    nb, t = hs.shape[:2]
    qs, ks, vs, lfs, bgs, us = _project(hs, w_in, b_forget)
    past = page_table.shape[1] * cache_k.shape[1]
    k_all = jnp.concatenate([cache_k[page_table].reshape(nb, past, N_HEADS, HEAD_DIM).astype(ks.dtype), ks], axis=1)
    v_all = jnp.concatenate([cache_v[page_table].reshape(nb, past, N_HEADS, HEAD_DIM).astype(vs.dtype), vs], axis=1)
    lf_all = jnp.concatenate([cache_logf[page_table].reshape(nb, past, N_HEADS).astype(jnp.float32), lfs], axis=1)
    c_all = jnp.cumsum(lf_all, axis=1)
    att_s = _fox_attend(qs, k_all, v_all, c_all[:, past:], c_all,
                        past + jnp.arange(t), jnp.arange(past + t)).reshape(nb, t, D_ATT)
    us_ext = jnp.concatenate([state_conv.astype(us.dtype), us], axis=1)
    conv_s = bgs * _causal_conv(us_ext, conv_w, t)
    mix_s = jnp.concatenate([att_s, conv_s], axis=-1) @ w_out
    hs_new = _channel_stage(hs, mix_s, ln1_g, ln1_b, moe_w, ln2_g, ln2_b)
    cs = us_ext[:, -CONV_STATE:]
    return hp_new, hs_new, kp, vp, lfp, cp, ks, vs, lfs, cs


def setup_inputs(seed: int = 0) -> dict:
    key = jax.random.key(seed)
    ks = jax.random.split(key, 26)
    n_pages = PAST_LEN // PAGE_SIZE
    n_pool = (5 * DEC_BATCH * n_pages) // 4
    nrm = jax.random.normal
    f32 = jnp.float32
    page_table = jax.random.permutation(ks[6], n_pool)[:DEC_BATCH * n_pages].reshape(DEC_BATCH, n_pages).astype(jnp.int32)
    return {
        "x_prompt": nrm(ks[0], (BATCH, SEQ, D_MODEL), f32),
        "x_sample": nrm(ks[1], (DEC_BATCH, DEC_SEQ, D_MODEL), f32),
        "cache_k": nrm(ks[2], (DEPTH, n_pool, PAGE_SIZE, N_HEADS, HEAD_DIM), f32),
        "cache_v": nrm(ks[3], (DEPTH, n_pool, PAGE_SIZE, N_HEADS, HEAD_DIM), f32),
        "cache_logf": jax.nn.log_sigmoid(FORGET_BIAS_INIT + nrm(ks[4], (DEPTH, n_pool, PAGE_SIZE, N_HEADS), f32)),
        "state_conv": nrm(ks[5], (DEPTH, DEC_BATCH, CONV_STATE, D_CONV), f32),
        "page_table": page_table,
        "meta_tokens": nrm(ks[7], (N_META, D_MODEL), f32),
        "ln_in_g": 1.0 + 0.02 * nrm(ks[8], (D_MODEL,), f32),
        "ln_in_b": 0.02 * nrm(ks[9], (D_MODEL,), f32),
        "w_in": nrm(ks[10], (DEPTH, D_MODEL, N_IN), f32) * D_MODEL ** -0.5,
        "b_forget": FORGET_BIAS_INIT + 0.1 * nrm(ks[11], (DEPTH, N_HEADS), f32),
        "conv_w": nrm(ks[12], (DEPTH, CONV_W, D_CONV), f32) * CONV_W ** -0.5,
        "w_out": nrm(ks[13], (DEPTH, D_MIX, D_MODEL), f32) * (D_MIX ** -0.5 * DN_BETA),
        "ln1_g": 1.0 + 0.02 * nrm(ks[14], (DEPTH, D_MODEL), f32),
        "ln1_b": 0.02 * nrm(ks[15], (DEPTH, D_MODEL), f32),
        "w_router": nrm(ks[16], (DEPTH, D_MODEL, N_EXPERTS), f32) * D_MODEL ** -0.5,
        "b_router": 0.01 * nrm(ks[17], (DEPTH, N_EXPERTS), f32),
        "w_gate": nrm(ks[18], (DEPTH, N_EXPERTS, D_MODEL, D_FF), f32) * D_MODEL ** -0.5,
        "b_gate": 0.01 * nrm(ks[19], (DEPTH, N_EXPERTS, D_FF), f32),
        "w_up": nrm(ks[20], (DEPTH, N_EXPERTS, D_MODEL, D_FF), f32) * D_MODEL ** -0.5,
        "b_up": 0.01 * nrm(ks[21], (DEPTH, N_EXPERTS, D_FF), f32),
        "w_down": nrm(ks[22], (DEPTH, N_EXPERTS, D_FF, D_MODEL), f32) * (D_FF ** -0.5 * DN_BETA),
        "b_down": 0.01 * nrm(ks[23], (DEPTH, N_EXPERTS, D_MODEL), f32),
        "ln2_g": 1.0 + 0.02 * nrm(ks[24], (DEPTH, D_MODEL), f32),
        "ln2_b": 0.02 * nrm(ks[25], (DEPTH, D_MODEL), f32),
    }


def reference(x_prompt, x_sample, cache_k, cache_v, cache_logf, state_conv, page_table, meta_tokens,
              ln_in_g, ln_in_b, w_in, b_forget, conv_w, w_out, ln1_g, ln1_b, w_router, b_router,
              w_gate, b_gate, w_up, b_up, w_down, b_down, ln2_g, ln2_b):
    bsz = x_prompt.shape[0]
    meta = jnp.broadcast_to(meta_tokens[None].astype(x_prompt.dtype), (bsz, N_META, D_MODEL))
    hp = _layer_norm(jnp.concatenate([meta, x_prompt], axis=1), ln_in_g, ln_in_b)
    hs = _layer_norm(x_sample, ln_in_g, ln_in_b)
    kp_l, vp_l, lfp_l, cp_l, ks_l, vs_l, lfs_l, cs_l = [], [], [], [], [], [], [], []
    for l in range(DEPTH):
        hp, hs, kp, vp, lfp, cp, ks, vs, lfs, cs = _layer(
            hp, hs, cache_k[l], cache_v[l], cache_logf[l], state_conv[l], page_table,
            w_in[l], b_forget[l], conv_w[l], w_out[l], ln1_g[l], ln1_b[l], w_router[l], b_router[l],
            w_gate[l], b_gate[l], w_up[l], b_up[l], w_down[l], b_down[l], ln2_g[l], ln2_b[l])
        kp_l.append(kp); vp_l.append(vp); lfp_l.append(lfp); cp_l.append(cp)
        ks_l.append(ks); vs_l.append(vs); lfs_l.append(lfs); cs_l.append(cs)
    y_prompt = hp[:, N_META:]
    y_sample = hs
    return (y_prompt, y_sample, jnp.stack(kp_l), jnp.stack(vp_l), jnp.stack(lfp_l), jnp.stack(cp_l),
            jnp.stack(ks_l), jnp.stack(vs_l), jnp.stack(lfs_l), jnp.stack(cs_l))
```

```python
import functools

import jax
import jax.numpy as jnp
from jax import lax
from jax.experimental import pallas as pl
from jax.experimental.pallas import tpu as pltpu

F32 = jnp.float32
BF16 = jnp.bfloat16
I32 = jnp.int32

D_MODEL = 2048
N_META = 16
D_ATT = 1024
D_CONV = 1024
HEAD_DIM = 128
N_HEADS = 8
CONV_W = 3
N_EXPERTS = 32
TOP_K = 4
D_FF = 2048
SWIGLU_LIMIT = 7.0
SWIGLU_ALPHA = 1.702
LN_EPS = 1e-5
DEPTH = 1
DN_ALPHA = (2 * DEPTH) ** 0.25
ATT_SCALE = HEAD_DIM ** -0.5
NEG = -1e30

LANES = 128
SUBLANES = 8
VMEM_LIMIT = 56 * 1024 * 1024

SMALL_ROWS = 64
SAMPLE_ROW0 = N_META

PROJ_TN = 512
N_PROJ_TILES = 12
MOE_TB = 512
MOE_TN = 512
MOE_NJ = D_FF // MOE_TN


def _cparams(sem):
    return pltpu.CompilerParams(dimension_semantics=sem, vmem_limit_bytes=VMEM_LIMIT)


def _layer_norm(x, g, b):
    mu = jnp.mean(x, axis=-1, keepdims=True)
    xc = x - mu
    var = jnp.mean(xc * xc, axis=-1, keepdims=True)
    return xc * lax.rsqrt(var + LN_EPS) * g + b


def _split2(x):
    hi = x.astype(BF16)
    lo = (x - hi.astype(F32)).astype(BF16)
    return hi, lo


def _split3(x):
    hi = x.astype(BF16)
    r = x - hi.astype(F32)
    mid = r.astype(BF16)
    lo = (r - mid.astype(F32)).astype(BF16)
    return hi, mid, lo


def _dot(a, b):
    return jnp.dot(a, b, preferred_element_type=F32)


def _dot_nt(a, b):
    return lax.dot_general(a, b, (((1,), (1,)), ((), ())), preferred_element_type=F32)


def _proj_kernel(x_ref, g_ref, b_ref, w_ref, wf_ref, bf_ref,
                 zf_ref, zb_ref, bg_ref, u_ref, lf_ref, hb_scr):
    j = pl.program_id(1)

    @pl.when(j == 0)
    def _():
        h = _layer_norm(x_ref[...], g_ref[...], b_ref[...])
        hb = h.astype(BF16)
        hb_scr[...] = hb
        fl = _dot(hb, wf_ref[...]) + bf_ref[...]
        lf_ref[...] = jnp.minimum(fl, 0.0) - jnp.log1p(jnp.exp(-jnp.abs(fl)))

    z = _dot(hb_scr[...], w_ref[...])

    @pl.when(j < 6)
    def _():
        zb_ref[...] = z.astype(BF16)

    @pl.when((j >= 2) & (j < 6))
    def _():
        zf_ref[...] = z

    @pl.when((j >= 6) & (j < 8))
    def _():
        bg_ref[...] = z

    @pl.when(j >= 8)
    def _():
        half = PROJ_TN // 2
        u_ref[...] = z[:, :half] * z[:, half:]


def _project(x, g, b, wp, wf, bfp, tm):
    rows = x.shape[0]
    grid = (rows // tm, N_PROJ_TILES)
    return pl.pallas_call(
        _proj_kernel,
        grid=grid,
        in_specs=[
            pl.BlockSpec((tm, D_MODEL), lambda i, j: (i, 0)),
            pl.BlockSpec((1, D_MODEL), lambda i, j: (0, 0)),
            pl.BlockSpec((1, D_MODEL), lambda i, j: (0, 0)),
            pl.BlockSpec((D_MODEL, PROJ_TN), lambda i, j: (0, j)),
            pl.BlockSpec((D_MODEL, LANES), lambda i, j: (0, 0)),
            pl.BlockSpec((1, LANES), lambda i, j: (0, 0)),
        ],
        out_specs=[
            pl.BlockSpec((tm, PROJ_TN), lambda i, j: (i, jnp.clip(j - 2, 0, 3))),
            pl.BlockSpec((tm, PROJ_TN), lambda i, j: (i, jnp.minimum(j, 5))),
            pl.BlockSpec((tm, PROJ_TN), lambda i, j: (i, jnp.clip(j - 6, 0, 1))),
            pl.BlockSpec((tm, PROJ_TN // 2), lambda i, j: (i, jnp.clip(j - 8, 0, 3))),
            pl.BlockSpec((tm, LANES), lambda i, j: (i, 0)),
        ],
        out_shape=[
            jax.ShapeDtypeStruct((rows, 2 * D_ATT), F32),
            jax.ShapeDtypeStruct((rows, 3 * D_ATT), BF16),
            jax.ShapeDtypeStruct((rows, D_CONV), F32),
            jax.ShapeDtypeStruct((rows, D_CONV), F32),
            jax.ShapeDtypeStruct((rows, LANES), F32),
        ],
        scratch_shapes=[pltpu.VMEM((tm, D_MODEL), BF16)],
        compiler_params=_cparams(("arbitrary", "arbitrary")),
        name="ln_in_proj",
    )(x, g, b, wp, wf, bfp)


ATT_T = 512


def _att_update(h, q_ref, k, v, bias_row, cq_ref, m_scr, l_scr, acc_scr, masked, first):
    hs = slice(h * HEAD_DIM, (h + 1) * HEAD_DIM)
    q = q_ref[0, :, hs]
    s = _dot_nt(q, k) * ATT_SCALE + (cq_ref[0, :, h:h + 1] - bias_row)
    if masked:
        row = lax.broadcasted_iota(I32, s.shape, 0)
        col = lax.broadcasted_iota(I32, s.shape, 1)
        s = jnp.where(col <= row, s, NEG)
    s_max = jnp.max(s, axis=1, keepdims=True)
    if first:
        m_new = s_max
        p = jnp.exp(s - m_new)
        l_new = jnp.sum(p, axis=1, keepdims=True)
        acc_new = _dot(p.astype(BF16), v)
    else:
        m_prev = m_scr[:, h:h + 1]
        m_new = jnp.maximum(m_prev, s_max)
        alpha = jnp.exp(m_prev - m_new)
        p = jnp.exp(s - m_new)
        l_new = alpha * l_scr[:, h:h + 1] + jnp.sum(p, axis=1, keepdims=True)
        acc_new = alpha * acc_scr[:, hs] + _dot(p.astype(BF16), v)
    m_scr[:, h:h + 1] = m_new
    l_scr[:, h:h + 1] = l_new
    acc_scr[:, hs] = acc_new


def _flash_kernel(q_ref, k_ref, v_ref, kvm_ref, cq_ref, ck_ref, ckm_ref, o_ref,
                  m_scr, l_scr, acc_scr):
    qi = pl.program_id(1)
    kj = pl.program_id(2)

    @pl.when(kj == 0)
    def _():
        for h in range(N_HEADS):
            hs = slice(h * HEAD_DIM, (h + 1) * HEAD_DIM)
            vs = slice(D_ATT + h * HEAD_DIM, D_ATT + (h + 1) * HEAD_DIM)
            _att_update(h, q_ref, kvm_ref[:, hs], kvm_ref[:, vs], ckm_ref[h:h + 1, :],
                        cq_ref, m_scr, l_scr, acc_scr, masked=False, first=True)

    def body(masked):
        for h in range(N_HEADS):
            hs = slice(h * HEAD_DIM, (h + 1) * HEAD_DIM)
            _att_update(h, q_ref, k_ref[0, :, hs], v_ref[0, :, hs], ck_ref[0, h:h + 1, :],
                        cq_ref, m_scr, l_scr, acc_scr, masked=masked, first=False)

    @pl.when(kj < qi)
    def _():
        body(False)

    @pl.when(kj == qi)
    def _():
        body(True)
        for h in range(N_HEADS):
            hs = slice(h * HEAD_DIM, (h + 1) * HEAD_DIM)
            o_ref[0, :, hs] = (acc_scr[:, hs] / l_scr[:, h:h + 1]).astype(BF16)


def _prompt_attention(zb, kvm, cq, ck, ckm):
    bsz, seq = zb.shape[0], zb.shape[1]
    nt = seq // ATT_T
    kv_idx = lambda off: (lambda b, i, j: (b, jnp.minimum(j, i), off))
    return pl.pallas_call(
        _flash_kernel,
        grid=(bsz, nt, nt),
        in_specs=[
            pl.BlockSpec((1, ATT_T, D_ATT), lambda b, i, j: (b, i, 0)),
            pl.BlockSpec((1, ATT_T, D_ATT), kv_idx(1)),
            pl.BlockSpec((1, ATT_T, D_ATT), kv_idx(2)),
            pl.BlockSpec((LANES, 2 * D_ATT), lambda b, i, j: (0, 0)),
            pl.BlockSpec((1, ATT_T, N_HEADS), lambda b, i, j: (b, i, 0)),
            pl.BlockSpec((1, N_HEADS, ATT_T), lambda b, i, j: (b, 0, jnp.minimum(j, i))),
            pl.BlockSpec((N_HEADS, LANES), lambda b, i, j: (0, 0)),
        ],
        out_specs=pl.BlockSpec((1, ATT_T, D_ATT), lambda b, i, j: (b, i, 0)),
        out_shape=jax.ShapeDtypeStruct((bsz, seq, D_ATT), BF16),
        scratch_shapes=[
            pltpu.VMEM((ATT_T, LANES), F32),
            pltpu.VMEM((ATT_T, LANES), F32),
            pltpu.VMEM((ATT_T, D_ATT), F32),
        ],
        compiler_params=_cparams(("arbitrary", "arbitrary", "arbitrary")),
        name="prompt_attention",
    )(zb, zb, zb, kvm, cq, ck, ckm)


PAGES_PER_STEP = 4
PAGE = 128
FLAT = PAGE * N_HEADS
QROWS = 32


def _dot3_left(x, m):
    hi, mid, lo = _split3(x)
    return _dot(hi, m) + _dot(mid, m) + _dot(lo, m)


def _dot3_right(m, x):
    hi, mid, lo = _split3(x)
    return _dot(m, hi) + _dot(m, mid) + _dot(m, lo)


def _logf_pages_kernel(pt_ref, lf_hbm, mcum_ref, mtot_ref, tri_ref, ck_ref, x_scr, sem):
    b = pl.program_id(0)
    n_pages = x_scr.shape[0]

    def copy(j):
        return pltpu.make_async_copy(lf_hbm.at[pl.ds(pt_ref[b, j], 1)], x_scr.at[pl.ds(j, 1)], sem)

    def start(j, c):
        copy(j).start()
        return c

    def wait(j, c):
        copy(j).wait()
        return c

    lax.fori_loop(0, n_pages, start, 0)
    lax.fori_loop(0, n_pages, wait, 0)
    x = x_scr[...]
    within = _dot3_left(x, mcum_ref[...])
    tot = _dot3_left(x, mtot_ref[...])
    carry = _dot3_right(tri_ref[...], tot)
    ck_ref[0] = within + carry


def _logf_pages(page_table, lf_flat, mcum, mtot, tri):
    nb, n_pages = page_table.shape
    grid_spec = pltpu.PrefetchScalarGridSpec(
        num_scalar_prefetch=1,
        grid=(nb,),
        in_specs=[
            pl.BlockSpec(memory_space=pl.ANY),
            pl.BlockSpec((FLAT, FLAT), lambda b, pt: (0, 0)),
            pl.BlockSpec((FLAT, FLAT), lambda b, pt: (0, 0)),
            pl.BlockSpec((n_pages, n_pages), lambda b, pt: (0, 0)),
        ],
        out_specs=pl.BlockSpec((1, n_pages, FLAT), lambda b, pt: (b, 0, 0)),
        scratch_shapes=[pltpu.VMEM((n_pages, FLAT), F32), pltpu.SemaphoreType.DMA(())],
    )
    return pl.pallas_call(
        _logf_pages_kernel,
        grid_spec=grid_spec,
        out_shape=jax.ShapeDtypeStruct((nb, n_pages, FLAT), F32),
        compiler_params=_cparams(("arbitrary",)),
        name="paged_logf_cumsum",
    )(page_table, lf_flat, mcum, mtot, tri)


def _paged_kernel(pt_ref, *refs):
    g = PAGES_PER_STEP
    k_refs = refs[:g]
    v_refs = refs[g:2 * g]
    q_ref, ck_ref, cq_ref, kn_ref, vn_ref, cn_ref, o_ref, m_scr, l_scr, acc_scr = refs[2 * g:]
    j = pl.program_id(1)

    @pl.when(j == 0)
    def _():
        m_scr[...] = jnp.full(m_scr.shape, NEG, F32)
        l_scr[...] = jnp.zeros(l_scr.shape, F32)
        acc_scr[...] = jnp.zeros(acc_scr.shape, F32)

    q = q_ref[0]
    cq = cq_ref[0]

    def update(k2d, v2d, ck_row, extra_mask):
        s = _dot_nt(q, k2d) * ATT_SCALE + (cq - ck_row)
        row = lax.broadcasted_iota(I32, s.shape, 0)
        col = lax.broadcasted_iota(I32, s.shape, 1)
        keep = (col % N_HEADS) == (row % N_HEADS)
        if extra_mask:
            keep = keep & ((col // N_HEADS) <= (row // N_HEADS))
        s = jnp.where(keep, s, NEG)
        m_prev = m_scr[...]
        m_new = jnp.maximum(m_prev, jnp.max(s, axis=1, keepdims=True))
        alpha = jnp.exp(m_prev - m_new)
        p = jnp.where(keep, jnp.exp(s - m_new), 0.0)
        l_scr[...] = alpha * l_scr[...] + jnp.sum(p, axis=1, keepdims=True)
        acc_scr[...] = alpha * acc_scr[...] + _dot(p.astype(BF16), v2d)
        m_scr[...] = m_new

    for i in range(g):
        k2d = k_refs[i][...].reshape(FLAT, HEAD_DIM).astype(BF16)
        v2d = v_refs[i][...].reshape(FLAT, HEAD_DIM).astype(BF16)
        update(k2d, v2d, ck_ref[0, 0, i:i + 1, :], False)

    @pl.when(j == pl.num_programs(1) - 1)
    def _():
        update(kn_ref[0], vn_ref[0], cn_ref[0], True)
        o_ref[0] = acc_scr[...] / l_scr[...]


def _paged_attention(page_table, cache_k, cache_v, q, ck, cq, kn, vn, cn):
    nb, n_pages = page_table.shape
    g = PAGES_PER_STEP
    steps = n_pages // g
    page_block = (None, None, PAGE, N_HEADS, HEAD_DIM)

    def page_spec(i):
        return pl.BlockSpec(page_block, lambda b, j, pt: (0, pt[b, j * g + i], 0, 0, 0))

    per_b = lambda shape: pl.BlockSpec((1,) + shape, lambda b, j, pt: (b, 0, 0))
    grid_spec = pltpu.PrefetchScalarGridSpec(
        num_scalar_prefetch=1,
        grid=(nb, steps),
        in_specs=[page_spec(i) for i in range(g)] + [page_spec(i) for i in range(g)] + [
            per_b((QROWS, HEAD_DIM)),
            pl.BlockSpec((1, 1, g, FLAT), lambda b, j, pt: (b, j, 0, 0)),
            per_b((QROWS, 1)),
            per_b((QROWS, HEAD_DIM)),
            per_b((QROWS, HEAD_DIM)),
            per_b((1, QROWS)),
        ],
        out_specs=per_b((QROWS, HEAD_DIM)),
        scratch_shapes=[
            pltpu.VMEM((QROWS, 1), F32),
            pltpu.VMEM((QROWS, 1), F32),
            pltpu.VMEM((QROWS, HEAD_DIM), F32),
        ],
    )
    return pl.pallas_call(
        _paged_kernel,
        grid_spec=grid_spec,
        out_shape=jax.ShapeDtypeStruct((nb, QROWS, HEAD_DIM), F32),
        compiler_params=_cparams(("arbitrary", "arbitrary")),
        name="paged_attention",
    )(page_table, *([cache_k] * g), *([cache_v] * g), q, ck.reshape(nb, steps, g, FLAT), cq, kn, vn, cn)


def _mix_tail(att, convp, x, lg, lb, wo_ref, g1, b1, wrh_ref, wrl_ref, br,
              x1_ref, idx_ref, gate_ref):
    mix = _dot(att, wo_ref[:D_ATT, :]) + _dot(convp.astype(BF16), wo_ref[D_ATT:, :])
    h = _layer_norm(x, lg, lb)
    x1 = _layer_norm(DN_ALPHA * h + mix, g1, b1)
    x1_ref[...] = x1
    xh, xl = _split2(x1)
    logits = _dot(xh, wrh_ref[...]) + _dot(xl, wrh_ref[...]) + _dot(xh, wrl_ref[...]) + br
    lane = lax.broadcasted_iota(I32, logits.shape, 1)
    lane_f = lane.astype(F32)
    idx_out = jnp.zeros(logits.shape, I32)
    val_out = jnp.zeros(logits.shape, F32)
    v0 = None
    den = None
    cur = logits
    for k in range(TOP_K):
        m = jnp.max(cur, axis=1, keepdims=True)
        idx = jnp.min(jnp.where(cur == m, lane_f, float(LANES)), axis=1, keepdims=True).astype(I32)
        if k == 0:
            v0 = m
            e = jnp.ones_like(m)
            den = e
        else:
            e = jnp.exp(m - v0)
            den = den + e
        idx_out = jnp.where(lane == k, idx, idx_out)
        val_out = jnp.where(lane == k, e, val_out)
        cur = jnp.where(lane == idx, -jnp.inf, cur)
    idx_ref[...] = idx_out
    gate_ref[...] = val_out / den


def _mix_kernel_halo(att_ref, bg_ref, u_ref, hp_ref, hm_ref, x_ref, lg_ref, lb_ref, cw_ref,
                     wo_ref, g1_ref, b1_ref, wrh_ref, wrl_ref, br_ref, x1s_ref, idxs_ref, gates_ref,
                     x1_ref, idx_ref, gate_ref, ext_scr, *, tiles_per_seq, n_tiles):
    i = pl.program_id(0)
    tm = u_ref.shape[0]

    @pl.when(i < n_tiles)
    def _():
        u = u_ref[...]
        is_start = (i % tiles_per_seq) == 0
        ext_scr[0:SUBLANES, :] = jnp.where(is_start, hm_ref[...], hp_ref[...])
        ext_scr[SUBLANES:, :] = u
        um1 = ext_scr[pl.ds(SUBLANES - 1, tm), :]
        um2 = ext_scr[pl.ds(SUBLANES - 2, tm), :]
        conv = um2 * cw_ref[0:1, :] + um1 * cw_ref[1:2, :] + u * cw_ref[2:3, :]
        _mix_tail(att_ref[...], bg_ref[...] * conv, x_ref[...], lg_ref[...], lb_ref[...], wo_ref,
                  g1_ref[...], b1_ref[...], wrh_ref, wrl_ref, br_ref[...], x1_ref, idx_ref, gate_ref)

    @pl.when(i == n_tiles)
    def _():
        for src, dst in ((x1s_ref, x1_ref), (idxs_ref, idx_ref), (gates_ref, gate_ref)):
            dst[0:SMALL_ROWS, :] = src[...]
            dst[SMALL_ROWS:, :] = jnp.zeros((tm - SMALL_ROWS, dst.shape[1]), dst.dtype)


def _mix_kernel_shifted(att_ref, bg_ref, u_ref, um1_ref, um2_ref, x_ref, lg_ref, lb_ref, cw_ref,
                        wo_ref, g1_ref, b1_ref, wrh_ref, wrl_ref, br_ref,
                        x1_ref, idx_ref, gate_ref):
    conv = um2_ref[...] * cw_ref[0:1, :] + um1_ref[...] * cw_ref[1:2, :] + u_ref[...] * cw_ref[2:3, :]
    _mix_tail(att_ref[...], bg_ref[...] * conv, x_ref[...], lg_ref[...], lb_ref[...], wo_ref,
              g1_ref[...], b1_ref[...], wrh_ref, wrl_ref, br_ref[...], x1_ref, idx_ref, gate_ref)


def _const_spec(shape):
    return pl.BlockSpec(shape, lambda i: (0,) * len(shape))


def _mix_weight_specs():
    return [
        _const_spec((1, D_MODEL)), _const_spec((1, D_MODEL)), _const_spec((CONV_W, D_CONV)),
        _const_spec((D_MODEL, D_MODEL)), _const_spec((1, D_MODEL)), _const_spec((1, D_MODEL)),
        _const_spec((D_MODEL, LANES)), _const_spec((D_MODEL, LANES)), _const_spec((1, LANES)),
    ]


MIX_TM = 256


def _mix_big(att, bg, u, u_small, x, weights, small_out, seq):
    rows = x.shape[0]
    tm = MIX_TM
    n_tiles = rows // tm
    row = lambda w: pl.BlockSpec((tm, w), lambda i: (jnp.minimum(i, n_tiles - 1), 0))
    out_row = lambda w: pl.BlockSpec((tm, w), lambda i: (i, 0))
    kernel = functools.partial(_mix_kernel_halo, tiles_per_seq=seq // tm, n_tiles=n_tiles)
    total_rows = rows + tm
    return pl.pallas_call(
        kernel,
        grid=(n_tiles + 1,),
        in_specs=[
            row(D_ATT), row(D_CONV), row(D_CONV),
            pl.BlockSpec((SUBLANES, D_CONV),
                         lambda i: (jnp.maximum(jnp.minimum(i, n_tiles - 1) * (tm // SUBLANES) - 1, 0), 0)),
            pl.BlockSpec((SUBLANES, D_CONV), lambda i: (N_META // SUBLANES - 1, 0)),
            row(D_MODEL),
        ] + _mix_weight_specs() + [
            _const_spec((SMALL_ROWS, D_MODEL)), _const_spec((SMALL_ROWS, LANES)),
            _const_spec((SMALL_ROWS, LANES)),
        ],
        out_specs=[out_row(D_MODEL), out_row(LANES), out_row(LANES)],
        out_shape=[
            jax.ShapeDtypeStruct((total_rows, D_MODEL), F32),
            jax.ShapeDtypeStruct((total_rows, LANES), I32),
            jax.ShapeDtypeStruct((total_rows, LANES), F32),
        ],
        scratch_shapes=[pltpu.VMEM((tm + SUBLANES, D_CONV), F32)],
        compiler_params=_cparams(("arbitrary",)),
        name="mix_ln_router",
    )(att, bg, u, u, u_small, x, *weights, *small_out)


def _mix_small(att, bg, u, um1, um2, x, weights):
    tm = SMALL_ROWS
    row = lambda w: pl.BlockSpec((tm, w), lambda i: (0, 0))
    return pl.pallas_call(
        _mix_kernel_shifted,
        grid=(1,),
        in_specs=[row(D_ATT), row(D_CONV), row(D_CONV), row(D_CONV), row(D_CONV), row(D_MODEL)]
        + _mix_weight_specs(),
        out_specs=[row(D_MODEL), row(LANES), row(LANES)],
        out_shape=[
            jax.ShapeDtypeStruct((tm, D_MODEL), F32),
            jax.ShapeDtypeStruct((tm, LANES), I32),
            jax.ShapeDtypeStruct((tm, LANES), F32),
        ],
        compiler_params=_cparams(("arbitrary",)),
        name="mix_ln_router_small",
    )(att, bg, u, um1, um2, x, *weights)


def _gather_kernel(nv_ref, tok_ref, x_hbm, zero_hbm, xs_hbm, sem):
    c = pl.program_id(0)
    tb = tok_ref.shape[2]

    @pl.when(c >= nv_ref[0])
    def _():
        fill = pltpu.make_async_copy(zero_hbm, xs_hbm.at[pl.ds(c * tb, tb)], sem)
        fill.start()
        fill.wait()

    @pl.when(c < nv_ref[0])
    def _():
        def copy(r):
            return pltpu.make_async_copy(x_hbm.at[pl.ds(tok_ref[0, 0, r], 1)],
                                         xs_hbm.at[pl.ds(c * tb + r, 1)], sem)

        def start(r, carry):
            copy(r).start()
            return carry

        def wait(r, carry):
            copy(r).wait()
            return carry

        lax.fori_loop(0, tb, start, 0)
        lax.fori_loop(0, tb, wait, 0)


def _moe_gather(nvalid, row_tok, x1_all, n_blocks):
    tb = MOE_TB
    grid_spec = pltpu.PrefetchScalarGridSpec(
        num_scalar_prefetch=1,
        grid=(n_blocks,),
        in_specs=[
            pl.BlockSpec((1, 1, tb), lambda c, nv: (c, 0, 0), memory_space=pltpu.SMEM),
            pl.BlockSpec(memory_space=pl.ANY),
            pl.BlockSpec(memory_space=pl.ANY),
        ],
        out_specs=pl.BlockSpec(memory_space=pl.ANY),
        scratch_shapes=[pltpu.SemaphoreType.DMA(())],
    )
    return pl.pallas_call(
        _gather_kernel,
        grid_spec=grid_spec,
        out_shape=jax.ShapeDtypeStruct((n_blocks * tb, D_MODEL), F32),
        compiler_params=_cparams(("arbitrary",)),
        name="moe_gather",
    )(nvalid, row_tok.reshape(n_blocks, 1, tb), x1_all, jnp.zeros((tb, D_MODEL), F32))


def _expert_kernel(nv_ref, be_ref, xs_ref, wg_ref, bgt_ref, wu_ref, bup_ref, wd_ref, bd_ref,
                   y_ref, xb_scr, hid_scr):
    c = pl.program_id(0)
    s = pl.program_id(1)
    valid = c < nv_ref[0]

    @pl.when(valid & (s == 0))
    def _():
        xb_scr[...] = xs_ref[...].astype(BF16)

    @pl.when(valid & (s < MOE_NJ))
    def _():
        xb = xb_scr[...]
        g = _dot(xb, wg_ref[0].astype(BF16)) + bgt_ref[0]
        u = _dot(xb, wu_ref[0].astype(BF16)) + bup_ref[0]
        g = jnp.minimum(g, SWIGLU_LIMIT)
        u = jnp.clip(u, -SWIGLU_LIMIT, SWIGLU_LIMIT)
        hid = (u + 1.0) * (g * (1.0 / (1.0 + jnp.exp(-SWIGLU_ALPHA * g))))
        hid_scr[s] = hid.astype(BF16)

    @pl.when(valid & (s >= MOE_NJ))
    def _():
        y = bd_ref[0] + _dot(hid_scr[0], wd_ref[0, 0:MOE_TN, :].astype(BF16))
        for k in range(1, MOE_NJ):
            y = y + _dot(hid_scr[k], wd_ref[0, k * MOE_TN:(k + 1) * MOE_TN, :].astype(BF16))
        y_ref[...] = y

    @pl.when(jnp.logical_not(valid) & (s >= MOE_NJ))
    def _():
        y_ref[...] = jnp.zeros(y_ref.shape, F32)


def _moe_experts(nvalid, blk_e, xs, w_gate, b_gate, w_up, b_up, w_down, b_down, n_blocks):
    tb, tn, nj = MOE_TB, MOE_TN, MOE_NJ

    def blk(c, nv):
        return jnp.minimum(c, nv[0] - 1)

    def up_j(c, s, nv):
        return jnp.where(c < nv[0], jnp.minimum(s, nj - 1), nj - 1)

    def down_j(c, s, nv):
        return jnp.where(c < nv[0], jnp.maximum(s - nj, 0), nj - 1)

    w_up_spec = pl.BlockSpec((1, D_MODEL, tn), lambda c, s, nv, be: (be[blk(c, nv)], 0, up_j(c, s, nv)))
    b_up_spec = pl.BlockSpec((1, 1, tn), lambda c, s, nv, be: (be[blk(c, nv)], 0, up_j(c, s, nv)))
    grid_spec = pltpu.PrefetchScalarGridSpec(
        num_scalar_prefetch=2,
        grid=(n_blocks, 2 * nj),
        in_specs=[
            pl.BlockSpec((tb, D_MODEL), lambda c, s, nv, be: (blk(c, nv), 0)),
            w_up_spec, b_up_spec, w_up_spec, b_up_spec,
            pl.BlockSpec((1, D_FF, tn), lambda c, s, nv, be: (be[blk(c, nv)], 0, down_j(c, s, nv))),
            pl.BlockSpec((1, 1, tn), lambda c, s, nv, be: (be[blk(c, nv)], 0, down_j(c, s, nv))),
        ],
        out_specs=pl.BlockSpec((tb, tn), lambda c, s, nv, be: (c, jnp.maximum(s - nj, 0))),
        scratch_shapes=[pltpu.VMEM((tb, D_MODEL), BF16), pltpu.VMEM((nj, tb, tn), BF16)],
    )
    return pl.pallas_call(
        _expert_kernel,
        grid_spec=grid_spec,
        out_shape=jax.ShapeDtypeStruct((n_blocks * tb, D_MODEL), F32),
        compiler_params=_cparams(("arbitrary", "arbitrary")),
        name="moe_experts",
    )(nvalid, blk_e, xs, w_gate, b_gate.reshape(N_EXPERTS, 1, D_FF), w_up,
      b_up.reshape(N_EXPERTS, 1, D_FF), w_down, b_down.reshape(N_EXPERTS, 1, D_MODEL))


def _combine_kernel(dest_ref, yb_hbm, gate_ref, x1_ref, g2_ref, b2_ref, o_ref, buf, sem):
    tm = x1_ref.shape[0]

    def copy(r, k):
        return pltpu.make_async_copy(yb_hbm.at[pl.ds(dest_ref[0, 0, r * TOP_K + k], 1)],
                                     buf.at[k, pl.ds(r, 1)], sem)

    def start(r, carry):
        for k in range(TOP_K):
            copy(r, k).start()
        return carry

    def wait(r, carry):
        for k in range(TOP_K):
            copy(r, k).wait()
        return carry

    lax.fori_loop(0, tm, start, 0)
    lax.fori_loop(0, tm, wait, 0)
    gates = gate_ref[...]
    f = gates[:, 0:1] * buf[0]
    for k in range(1, TOP_K):
        f = f + gates[:, k:k + 1] * buf[k]
    o_ref[...] = _layer_norm(DN_ALPHA * x1_ref[...] + f, g2_ref[...], b2_ref[...])


def _moe_combine(dest, yb, gates_all, x1_all, g2, b2, row0, rows, tm):
    blk0 = row0 // tm
    n = rows // tm
    dest3 = dest.reshape(-1, 1, tm * TOP_K)
    return pl.pallas_call(
        _combine_kernel,
        grid=(n,),
        in_specs=[
            pl.BlockSpec((1, 1, tm * TOP_K), lambda i: (blk0 + i, 0, 0), memory_space=pltpu.SMEM),
            pl.BlockSpec(memory_space=pl.ANY),
            pl.BlockSpec((tm, LANES), lambda i: (blk0 + i, 0)),
            pl.BlockSpec((tm, D_MODEL), lambda i: (blk0 + i, 0)),
            _const_spec((1, D_MODEL)), _const_spec((1, D_MODEL)),
        ],
        out_specs=pl.BlockSpec((tm, D_MODEL), lambda i: (i, 0)),
        out_shape=jax.ShapeDtypeStruct((rows, D_MODEL), F32),
        scratch_shapes=[pltpu.VMEM((TOP_K, tm, D_MODEL), F32), pltpu.SemaphoreType.DMA(())],
        compiler_params=_cparams(("arbitrary",)),
        name="moe_combine_ln",
    )(dest3, yb, gates_all, x1_all, g2, b2)


COMBINE_TM = 64


def _route(idx_all, valid, n_blocks):
    tb = MOE_TB
    n_tok = idx_all.shape[0]
    e = jnp.where(valid[:, None], idx_all[:, :TOP_K], N_EXPERTS).reshape(-1)
    onehot = (e[:, None] == jnp.arange(N_EXPERTS, dtype=I32)[None, :]).astype(I32)
    csum = jnp.cumsum(onehot, axis=0)
    counts = csum[-1]
    ec = jnp.minimum(e, N_EXPERTS - 1)
    rank = jnp.take_along_axis(csum, ec[:, None], axis=1)[:, 0] - 1
    pcounts = (counts + tb - 1) // tb * tb
    pend = jnp.cumsum(pcounts)
    pstart = pend - pcounts
    is_valid = e < N_EXPERTS
    dest = jnp.where(is_valid, pstart[ec] + rank, 0).astype(I32)
    nvalid = (pend[-1] // tb).astype(I32).reshape(1)
    blk_e = jnp.minimum(jnp.searchsorted(pend, jnp.arange(n_blocks, dtype=I32) * tb, side='right'),
                        N_EXPERTS - 1).astype(I32)
    tok = jnp.arange(n_tok * TOP_K, dtype=I32) // TOP_K
    scatter_to = jnp.where(is_valid, dest, n_blocks * tb)
    row_tok = jnp.zeros((n_blocks * tb,), I32).at[scatter_to].set(tok, mode='drop')
    return dest.reshape(n_tok, TOP_K), row_tok, blk_e, nvalid


def kernel(x_prompt, x_sample, cache_k, cache_v, cache_logf, state_conv, page_table, meta_tokens,
           ln_in_g, ln_in_b, w_in, b_forget, conv_w, w_out, ln1_g, ln1_b, w_router, b_router,
           w_gate, b_gate, w_up, b_up, w_down, b_down, ln2_g, ln2_b):
    assert w_in.shape[0] == DEPTH
    bsz, seq, _ = x_prompt.shape
    nb, dec_seq, _ = x_sample.shape
    n_samp = nb * dec_seq
    big_rows = bsz * seq
    total_rows = big_rows + MIX_TM
    n_pages = page_table.shape[1]
    n_pool = cache_k.shape[1]
    assert N_META + n_samp <= SMALL_ROWS and dec_seq * N_HEADS == QROWS

    row2 = lambda v: v.reshape(1, -1)
    lg, lb = row2(ln_in_g), row2(ln_in_b)

    w = w_in[0]
    o_f = 3 * D_ATT
    o_b = o_f + N_HEADS
    o_c = o_b + D_CONV
    o_h = o_c + D_CONV
    half = PROJ_TN // 2
    wc = w[:, o_c:o_h].reshape(D_MODEL, D_CONV // half, half)
    wh = w[:, o_h:].reshape(D_MODEL, D_CONV // half, half)
    w_ch = jnp.concatenate([wc, wh], axis=2).reshape(D_MODEL, 2 * D_CONV)
    wp = jnp.concatenate([w[:, :o_f], w[:, o_b:o_c], w_ch], axis=1).astype(BF16)
    wf = jnp.pad(w[:, o_f:o_b], ((0, 0), (0, LANES - N_HEADS))).astype(BF16)
    bfp = jnp.pad(b_forget[0], (0, LANES - N_HEADS)).reshape(1, LANES)
    wo = w_out[0].astype(BF16)
    wr = jnp.pad(w_router[0], ((0, 0), (0, LANES - N_EXPERTS)))
    wr_hi = wr.astype(BF16)
    wr_lo = (wr - wr_hi.astype(F32)).astype(BF16)
    br = jnp.pad(b_router[0], (0, LANES - N_EXPERTS), constant_values=NEG).reshape(1, LANES)
    mix_weights = (lg, lb, conv_w[0], wo, row2(ln1_g[0]), row2(ln1_b[0]), wr_hi, wr_lo, br)

    x_big = x_prompt.reshape(big_rows, D_MODEL)
    x_small = jnp.concatenate([
        meta_tokens, x_sample.reshape(n_samp, D_MODEL),
        jnp.zeros((SMALL_ROWS - N_META - n_samp, D_MODEL), F32)], axis=0)
    zf_b, zb_b, bg_b, u_b, lf_b = _project(x_big, lg, lb, wp, wf, bfp, 512)
    zf_s, zb_s, bg_s, u_s, lf_s = _project(x_small, lg, lb, wp, wf, bfp, SMALL_ROWS)

    samp = slice(SAMPLE_ROW0, SAMPLE_ROW0 + n_samp)
    lf_meta = lf_s[:N_META, :N_HEADS]
    lf_x = lf_b[:, :N_HEADS].reshape(bsz, seq, N_HEADS)
    lf_samp = lf_s[samp, :N_HEADS].reshape(nb, dec_seq, N_HEADS)

    c_meta = jnp.cumsum(lf_meta, axis=0)
    c_x = c_meta[-1][None, None, :] + jnp.cumsum(lf_x, axis=1)
    ckm = jnp.pad(c_meta.T, ((0, 0), (0, LANES - N_META)), constant_values=-NEG)
    kvm = jnp.pad(zb_s[:N_META, D_ATT:], ((0, LANES - N_META), (0, 0)))
    att_b = _prompt_attention(zb_b.reshape(bsz, seq, 3 * D_ATT), kvm, c_x,
                              jnp.swapaxes(c_x, 1, 2), ckm)

    lf_flat = cache_logf[0].reshape(n_pool, FLAT)
    ci = jnp.arange(FLAT, dtype=I32)
    same_head = (ci[:, None] % N_HEADS) == (ci[None, :] % N_HEADS)
    mcum = (same_head & (ci[:, None] // N_HEADS <= ci[None, :] // N_HEADS)).astype(BF16)
    mtot = same_head.astype(BF16)
    pj = jnp.arange(n_pages, dtype=I32)
    tri = (pj[None, :] < pj[:, None]).astype(BF16)
    ck_pages = _logf_pages(page_table, lf_flat, mcum, mtot, tri)
    c_past = ck_pages[:, -1, FLAT - N_HEADS:]
    c_new = c_past[:, None, :] + jnp.cumsum(lf_samp, axis=1)
    c_new_flat = c_new.reshape(nb, QROWS)
    z_samp = zb_s[samp]
    to_rows = lambda a: a.reshape(nb, QROWS, HEAD_DIM)
    att_s = _paged_attention(
        page_table, cache_k, cache_v, to_rows(z_samp[:, :D_ATT]), ck_pages,
        c_new_flat.reshape(nb, QROWS, 1), to_rows(z_samp[:, D_ATT:2 * D_ATT]),
        to_rows(z_samp[:, 2 * D_ATT:]), c_new_flat.reshape(nb, 1, QROWS))
    att_small = jnp.zeros((SMALL_ROWS, D_ATT), BF16).at[samp].set(
        att_s.reshape(n_samp, D_ATT).astype(BF16))

    us = u_s[samp].reshape(nb, dec_seq, D_CONV)
    us_ext = jnp.concatenate([state_conv[0], us], axis=1)

    def shifted(k):
        body = us_ext[:, k:k + dec_seq].reshape(n_samp, D_CONV)
        return jnp.zeros((SMALL_ROWS, D_CONV), F32).at[samp].set(body)

    small_out = _mix_small(att_small, bg_s, u_s, shifted(1), shifted(0), x_small, mix_weights)
    x1_all, idx_all, gate_all = _mix_big(att_b.reshape(big_rows, D_ATT), bg_b, u_b, u_s, x_big,
                                         mix_weights, small_out, seq)

    n_assign = (big_rows + n_samp) * TOP_K
    n_blocks = -(-(n_assign + N_EXPERTS * (MOE_TB - 1)) // MOE_TB)
    r = jnp.arange(total_rows, dtype=I32)
    valid = (r < big_rows) | ((r >= big_rows + SAMPLE_ROW0) & (r < big_rows + SAMPLE_ROW0 + n_samp))
    dest, row_tok, blk_e, nvalid = _route(idx_all, valid, n_blocks)
    xs = _moe_gather(nvalid, row_tok, x1_all, n_blocks)
    yb = _moe_experts(nvalid, blk_e, xs, w_gate[0], b_gate[0], w_up[0], b_up[0],
                      w_down[0], b_down[0], n_blocks)
    g2, b2 = row2(ln2_g[0]), row2(ln2_b[0])
    y_big = _moe_combine(dest, yb, gate_all, x1_all, g2, b2, 0, big_rows, COMBINE_TM)
    y_small = _moe_combine(dest, yb, gate_all, x1_all, g2, b2, big_rows, SMALL_ROWS, COMBINE_TM)

    def with_meta(z_meta, z_x, width):
        m = jnp.broadcast_to(z_meta[None], (bsz, N_META, width))
        return jnp.concatenate([m, z_x.reshape(bsz, seq, width)], axis=1)

    heads = lambda a: a.reshape(a.shape[:-1] + (N_HEADS, HEAD_DIM))
    k_prompt = heads(with_meta(zf_s[:N_META, :D_ATT], zf_b[:, :D_ATT], D_ATT))[None]
    v_prompt = heads(with_meta(zf_s[:N_META, D_ATT:], zf_b[:, D_ATT:], D_ATT))[None]
    logf_prompt = with_meta(lf_meta, lf_x, N_HEADS)[None]
    conv_prompt = u_b.reshape(bsz, seq, D_CONV)[:, seq - (CONV_W - 1):][None]
    k_sample = heads(zf_s[samp, :D_ATT].reshape(nb, dec_seq, D_ATT))[None]
    v_sample = heads(zf_s[samp, D_ATT:].reshape(nb, dec_seq, D_ATT))[None]
    logf_sample = lf_samp[None]
    conv_sample = us_ext[:, -(CONV_W - 1):][None]
    y_prompt = y_big.reshape(bsz, seq, D_MODEL)
    y_sample = y_small[samp].reshape(nb, dec_seq, D_MODEL)
    return (y_prompt, y_sample, k_prompt, v_prompt, logf_prompt, conv_prompt,
            k_sample, v_sample, logf_sample, conv_sample)
```
